```python
import math
import jax, jax.numpy as jnp
from jax import lax
import numpy as np

D_MODEL = 1024
BATCH = 2
SEQ = 16384
DEPTH = 4

N_MIXERS = 3
N_FOX = (DEPTH + 2) // 3
N_MLA = (DEPTH + 1) // 3
N_RET = DEPTH // 3

N_META = 16
Q_BLOCK = 128
CHUNK = 128

FOX_HEAD_DIM = 128
FOX_HEADS = D_MODEL // FOX_HEAD_DIM
FOX_WIDTH = FOX_HEADS * FOX_HEAD_DIM
FOX_IN = 4 * FOX_WIDTH + FOX_HEADS
FORGET_BIAS_OFFSET = 2.0

MLA_NOPE = 128
MLA_ROPE = 64
MLA_V = 128
MLA_HEADS = D_MODEL // 128
MLA_Q_LORA = 384
MLA_KV_LORA = 256
MLA_WIDTH = MLA_HEADS * MLA_V
MLA_IN = MLA_Q_LORA + MLA_KV_LORA + MLA_ROPE + MLA_WIDTH
ROPE_BASE = 10000.0

RET_QK_DIM = 256
RET_V_DIM = 512
RET_HEADS = D_MODEL // RET_QK_DIM
RET_QK_WIDTH = RET_HEADS * RET_QK_DIM
RET_WIDTH = RET_HEADS * RET_V_DIM
RET_IN = 2 * RET_QK_WIDTH + 2 * RET_WIDTH

DEEPNORM_ALPHA = (2 * DEPTH) ** 0.25
DEEPNORM_BETA = (8 * DEPTH) ** -0.25
NORM_EPS = 1e-5
NEG_INF = -1e30

kernel_name = "hybrid_fox_mla_retention_deepnorm"


def layer_norm(x, g, b):
    xf = x.astype(jnp.float32)
    mu = jnp.mean(xf, axis=-1, keepdims=True)
    var = jnp.mean(jnp.square(xf - mu), axis=-1, keepdims=True)
    return ((xf - mu) * lax.rsqrt(var + NORM_EPS) * g + b).astype(x.dtype)


def rms_norm(x, g):
    xf = x.astype(jnp.float32)
    ms = jnp.mean(jnp.square(xf), axis=-1, keepdims=True)
    return (xf * lax.rsqrt(ms + NORM_EPS) * g).astype(x.dtype)


def rotary(t, pos, inv_freq):
    ang = pos.astype(jnp.float32)[:, None] * inv_freq[None, :]
    cos = jnp.cos(ang)[None, :, None, :]
    sin = jnp.sin(ang)[None, :, None, :]
    t1, t2 = jnp.split(t.astype(jnp.float32), 2, axis=-1)
    return jnp.concatenate([t1 * cos - t2 * sin, t2 * cos + t1 * sin], axis=-1).astype(t.dtype)


def causal_block_attention(q, k, v, scale, cum_log_f=None):
    B, L, H, dk = q.shape
    dv = v.shape[-1]
    n_real = L - N_META
    n_blocks = n_real // Q_BLOCK
    has_decay = cum_log_f is not None

    def attend(qb, q_pos, cq, kk, vv, ck):
        s = jnp.einsum('bqhd,bkhd->bhqk', qb, kk).astype(jnp.float32) * scale
        if has_decay:
            s = s + (jnp.moveaxis(cq, 1, 2)[..., :, None] - jnp.moveaxis(ck, 1, 2)[..., None, :])
        mask = jnp.arange(kk.shape[1])[None, :] <= q_pos[:, None]
        s = jnp.where(mask, s, NEG_INF)
        p = jax.nn.softmax(s, axis=-1).astype(vv.dtype)
        return jnp.einsum('bhqk,bkhd->bqhd', p, vv)

    c_meta = cum_log_f[:, :N_META] if has_decay else None
    out_meta = attend(q[:, :N_META], jnp.arange(N_META), c_meta,
                      k[:, :N_META], v[:, :N_META], c_meta)

    q_blocks = jnp.moveaxis(q[:, N_META:].reshape(B, n_blocks, Q_BLOCK, H, dk), 1, 0)
    q_pos_blocks = N_META + jnp.arange(n_blocks)[:, None] * Q_BLOCK + jnp.arange(Q_BLOCK)[None, :]
    if has_decay:
        c_blocks = jnp.moveaxis(cum_log_f[:, N_META:].reshape(B, n_blocks, Q_BLOCK, H), 1, 0)
        xs = (q_blocks, q_pos_blocks, c_blocks)
    else:
        xs = (q_blocks, q_pos_blocks)

    def block_fn(args):
        cq = args[2] if has_decay else None
        return attend(args[0], args[1], cq, k, v, cum_log_f)

    out_real = lax.map(block_fn, xs)
    out_real = jnp.moveaxis(out_real, 0, 1).reshape(B, n_real, H, dv)
    return jnp.concatenate([out_meta, out_real], axis=1)


def fox_mixer(h, w_in, b_f, w_out):
    B, L, _ = h.shape
    proj = h @ w_in
    q, k, v, z, f_logit = jnp.split(
        proj, [FOX_WIDTH, 2 * FOX_WIDTH, 3 * FOX_WIDTH, 4 * FOX_WIDTH], axis=-1)
    q = q.reshape(B, L, FOX_HEADS, FOX_HEAD_DIM)
    k = k.reshape(B, L, FOX_HEADS, FOX_HEAD_DIM)
    v = v.reshape(B, L, FOX_HEADS, FOX_HEAD_DIM)
    log_f = jax.nn.log_sigmoid((f_logit + b_f).astype(jnp.float32))
    cum_log_f = jnp.cumsum(log_f, axis=1)
    o = causal_block_attention(q, k, v, FOX_HEAD_DIM ** -0.5, cum_log_f)
    y = o.reshape(B, L, FOX_WIDTH) * jax.nn.silu(z)
    return y @ w_out


def mla_mixer(h, pos, w_in, q_norm_g, kv_norm_g, w_uq, w_ukv, w_out):
    B, L, _ = h.shape
    proj = h @ w_in
    c_q, c_kv, k_rope, z = jnp.split(
        proj, [MLA_Q_LORA, MLA_Q_LORA + MLA_KV_LORA, MLA_Q_LORA + MLA_KV_LORA + MLA_ROPE], axis=-1)
    q = (rms_norm(c_q, q_norm_g) @ w_uq).reshape(B, L, MLA_HEADS, MLA_NOPE + MLA_ROPE)
    kv = (rms_norm(c_kv, kv_norm_g) @ w_ukv).reshape(B, L, MLA_HEADS, MLA_NOPE + MLA_V)
    q_nope, q_rope = q[..., :MLA_NOPE], q[..., MLA_NOPE:]
    k_nope, v = kv[..., :MLA_NOPE], kv[..., MLA_NOPE:]
    inv_freq = ROPE_BASE ** (-jnp.arange(0, MLA_ROPE, 2, dtype=jnp.float32) / MLA_ROPE)
    q_rope = rotary(q_rope, pos, inv_freq)
    k_rope = rotary(k_rope[:, :, None, :], pos, inv_freq)
    q_full = jnp.concatenate([q_nope, q_rope], axis=-1)
    k_full = jnp.concatenate(
        [k_nope, jnp.broadcast_to(k_rope, (B, L, MLA_HEADS, MLA_ROPE))], axis=-1)
    o = causal_block_attention(q_full, k_full, v, (MLA_NOPE + MLA_ROPE) ** -0.5)
    y = o.reshape(B, L, MLA_WIDTH) * jax.nn.silu(z)
    return y @ w_out


def retention_chunk(qc, kc, vc, state, log_gamma):
    C = qc.shape[1]
    i = jnp.arange(C, dtype=jnp.float32)
    rel = i[:, None] - i[None, :]
    intra_decay = jnp.where(rel[None] >= 0, jnp.exp(rel[None] * log_gamma[:, None, None]), 0.0)
    qf, kf, vf = qc.astype(jnp.float32), kc.astype(jnp.float32), vc.astype(jnp.float32)
    s = jnp.einsum('bihd,bjhd->bhij', qf, kf) * intra_decay
    intra = jnp.einsum('bhij,bjhe->bihe', s, vf)
    q_decay = jnp.exp((i[:, None] + 1.0) * log_gamma[None, :])
    cross = jnp.einsum('bihd,bhde->bihe', qf, state) * q_decay[None, :, :, None]
    k_decay = jnp.exp((C - 1.0 - i)[:, None] * log_gamma[None, :])
    new_state = (jnp.exp(C * log_gamma)[None, :, None, None] * state
                 + jnp.einsum('bjhd,bjhe->bhde', kf * k_decay[None, :, :, None], vf))
    return intra + cross, new_state


def retention_mixer(h, pos, w_in, gn_g, w_out):
    B, L, _ = h.shape
    n_chunks = (L - N_META) // CHUNK
    proj = h @ w_in
    q, k, v, z = jnp.split(
        proj, [RET_QK_WIDTH, 2 * RET_QK_WIDTH, 2 * RET_QK_WIDTH + RET_WIDTH], axis=-1)
    q = q.reshape(B, L, RET_HEADS, RET_QK_DIM)
    k = k.reshape(B, L, RET_HEADS, RET_QK_DIM)
    v = v.reshape(B, L, RET_HEADS, RET_V_DIM)
    inv_freq = 1.0 / (ROPE_BASE ** jnp.linspace(0.0, 1.0, RET_QK_DIM // 2, dtype=jnp.float32))
    q = rotary(q, pos, inv_freq)
    k = rotary(k, pos, inv_freq) * RET_QK_DIM ** -0.5
    log_gamma = jnp.log1p(-jnp.exp2(-5.0 - jnp.arange(RET_HEADS, dtype=jnp.float32)))

    state0 = jnp.zeros((B, RET_HEADS, RET_QK_DIM, RET_V_DIM), jnp.float32)
    o_meta, state = retention_chunk(q[:, :N_META], k[:, :N_META], v[:, :N_META], state0, log_gamma)

    def to_chunks(t):
        return jnp.moveaxis(t[:, N_META:].reshape(B, n_chunks, CHUNK, RET_HEADS, t.shape[-1]), 1, 0)

    def step(carry, xs):
        o_c, new_carry = retention_chunk(xs[0], xs[1], xs[2], carry, log_gamma)
        return new_carry, o_c

    _, o_real = lax.scan(step, state, (to_chunks(q), to_chunks(k), to_chunks(v)))
    o_real = jnp.moveaxis(o_real, 0, 1).reshape(B, L - N_META, RET_HEADS, RET_V_DIM)
    o = jnp.concatenate([o_meta, o_real], axis=1)
    mu = jnp.mean(o, axis=-1, keepdims=True)
    var = jnp.mean(jnp.square(o - mu), axis=-1, keepdims=True)
    o = ((o - mu) * lax.rsqrt(var + NORM_EPS)).reshape(B, L, RET_WIDTH) * gn_g
    y = o.astype(h.dtype) * jax.nn.silu(z)
    return y @ w_out


def setup_inputs(seed: int = 0) -> dict:
    key = jax.random.key(seed)
    ks = jax.random.split(key, 20)
    f32 = jnp.float32

    def w(k, shape, fan_in, scale=1.0):
        return jax.random.normal(k, shape, f32) * (fan_in ** -0.5) * scale

    x = jax.random.normal(ks[0], (BATCH, SEQ, D_MODEL), f32)
    meta = jax.random.normal(ks[1], (N_META, D_MODEL), f32)

    fox_w_in = w(ks[2], (N_FOX, D_MODEL, FOX_IN), D_MODEL)
    fox_b_f = FORGET_BIAS_OFFSET + 0.1 * jax.random.normal(ks[3], (N_FOX, FOX_HEADS), f32)
    fox_w_out = w(ks[4], (N_FOX, FOX_WIDTH, D_MODEL), FOX_WIDTH, DEEPNORM_BETA)

    mla_w_in = w(ks[5], (N_MLA, D_MODEL, MLA_IN), D_MODEL)
    mla_q_norm = 1.0 + 0.05 * jax.random.normal(ks[6], (N_MLA, MLA_Q_LORA), f32)
    mla_kv_norm = 1.0 + 0.05 * jax.random.normal(ks[7], (N_MLA, MLA_KV_LORA), f32)
    mla_w_uq = w(ks[8], (N_MLA, MLA_Q_LORA, MLA_HEADS * (MLA_NOPE + MLA_ROPE)), MLA_Q_LORA)
    mla_w_ukv = w(ks[9], (N_MLA, MLA_KV_LORA, MLA_HEADS * (MLA_NOPE + MLA_V)), MLA_KV_LORA)
    mla_w_out = w(ks[10], (N_MLA, MLA_WIDTH, D_MODEL), MLA_WIDTH, DEEPNORM_BETA)

    ret_w_in = w(ks[11], (N_RET, D_MODEL, RET_IN), D_MODEL)
    ret_gn_g = 1.0 + 0.05 * jax.random.normal(ks[12], (N_RET, RET_WIDTH), f32)
    ret_w_out = w(ks[13], (N_RET, RET_WIDTH, D_MODEL), RET_WIDTH, DEEPNORM_BETA)

    ln_g = 1.0 + 0.05 * jax.random.normal(ks[14], (DEPTH, D_MODEL), f32)
    ln_b = 0.02 * jax.random.normal(ks[15], (DEPTH, D_MODEL), f32)

    return {"x": x, "meta": meta,
            "fox_w_in": fox_w_in, "fox_b_f": fox_b_f, "fox_w_out": fox_w_out,
            "mla_w_in": mla_w_in, "mla_q_norm": mla_q_norm, "mla_kv_norm": mla_kv_norm,
            "mla_w_uq": mla_w_uq, "mla_w_ukv": mla_w_ukv, "mla_w_out": mla_w_out,
            "ret_w_in": ret_w_in, "ret_gn_g": ret_gn_g, "ret_w_out": ret_w_out,
            "ln_g": ln_g, "ln_b": ln_b}


def reference(x, meta, fox_w_in, fox_b_f, fox_w_out, mla_w_in, mla_q_norm, mla_kv_norm,
              mla_w_uq, mla_w_ukv, mla_w_out, ret_w_in, ret_gn_g, ret_w_out, ln_g, ln_b):
    B = x.shape[0]
    meta_b = jnp.broadcast_to(meta[None].astype(x.dtype), (B, N_META, D_MODEL))
    h = jnp.concatenate([meta_b, x], axis=1)
    pos = jnp.arange(h.shape[1])
    for i in range(DEPTH):
        kind, j = i % N_MIXERS, i // N_MIXERS
        if kind == 0:
            y = fox_mixer(h, fox_w_in[j], fox_b_f[j], fox_w_out[j])
        elif kind == 1:
            y = mla_mixer(h, pos, mla_w_in[j], mla_q_norm[j], mla_kv_norm[j],
                          mla_w_uq[j], mla_w_ukv[j], mla_w_out[j])
        else:
            y = retention_mixer(h, pos, ret_w_in[j], ret_gn_g[j], ret_w_out[j])
        h = layer_norm(DEEPNORM_ALPHA * h + y, ln_g[i], ln_b[i])
    return h[:, N_META:]
```

```python
import functools

import numpy as np

import jax
import jax.numpy as jnp
from jax import lax
from jax.experimental import pallas as pl
from jax.experimental.pallas import tpu as pltpu

F32 = jnp.float32
BF16 = jnp.bfloat16

D_MODEL = 1024
DEPTH = 4
N_META = 16
HEADS = 8
HEAD_DIM = 128
MLA_ROPE = 64
MLA_Q_LORA = 384
MLA_KV_LORA = 256
RET_HEADS = 4
RET_QK = 256
RET_V = 512
FORGET_SPLIT = 8
ROPE_BASE = 10000.0
DEEPNORM_ALPHA = (2 * DEPTH) ** 0.25
NORM_EPS = 1e-5
NEG_INF = -1e30

LANES = 128
QK_AUG = 256
TILE = 512
PAD = TILE
META0 = PAD - N_META
META_CHUNK = 128
RET_CHUNK = 256
VMEM_LIMIT_BYTES = 56 * 1024 * 1024


def _params(*semantics):
    return pltpu.CompilerParams(dimension_semantics=semantics,
                                vmem_limit_bytes=VMEM_LIMIT_BYTES)


def _resident(shape):
    return pl.BlockSpec(shape, lambda *_: (0,) * len(shape))


def _dot(a, b):
    return jnp.dot(a, b, preferred_element_type=F32)


def _dot_nt(a, b):
    return lax.dot_general(a, b, (((1,), (1,)), ((), ())), preferred_element_type=F32)


def _silu(z):
    return z / (1.0 + jnp.exp(-z))


def _fox_inproj_kernel(h_ref, w_ref, wfh_ref, wfl_ref, bf_ref, tri_ref, e_ref,
                       qa_ref, ka_ref, v_ref, g_ref, carry_ref):
    i = pl.program_id(1)

    @pl.when(i == 0)
    def _():
        carry_ref[...] = jnp.zeros_like(carry_ref)

    x = h_ref[...]
    xh = x.astype(BF16)
    xl = (x - xh.astype(F32)).astype(BF16)

    f = _dot(xh, wfh_ref[...]) + _dot(xl, wfh_ref[...]) + _dot(xh, wfl_ref[...])
    f = f + bf_ref[...]
    log_f = jnp.minimum(f, 0.0) - jnp.log1p(jnp.exp(-jnp.abs(f)))
    row = i * TILE + lax.broadcasted_iota(jnp.int32, log_f.shape, 0)
    log_f = jnp.where(row >= META0, log_f, 0.0)
    c = jnp.dot(tri_ref[...], log_f, precision=lax.Precision.HIGHEST,
                preferred_element_type=F32) + carry_ref[...]
    carry_ref[...] = c[TILE - 1:TILE, :]

    p1 = c.astype(BF16).astype(F32)
    r1 = c - p1
    p2 = r1.astype(BF16).astype(F32)
    p3 = (r1 - p2).astype(BF16).astype(F32)
    lane = lax.broadcasted_iota(jnp.int32, c.shape, 1)
    pieces = jnp.where(lane < FORGET_SPLIT, p1,
                       jnp.where(lane < 2 * FORGET_SPLIT, p2,
                                 jnp.where(lane < 3 * FORGET_SPLIT, p3,
                                           jnp.where(lane == 3 * FORGET_SPLIT, 1.0, 0.0))))
    aux = _dot(pieces.astype(BF16), e_ref[...])

    width = HEADS * HEAD_DIM
    q = _dot(xh, w_ref[:, 0:width]) * (HEAD_DIM ** -0.5)
    k = _dot(xh, w_ref[:, width:2 * width])
    for hd in range(HEADS):
        src = slice(hd * HEAD_DIM, (hd + 1) * HEAD_DIM)
        lo = hd * QK_AUG
        qa_ref[:, lo:lo + HEAD_DIM] = q[:, src].astype(BF16)
        qa_ref[:, lo + HEAD_DIM:lo + QK_AUG] = aux[:, src].astype(BF16)
        ka_ref[:, lo:lo + HEAD_DIM] = k[:, src].astype(BF16)
        ka_ref[:, lo + HEAD_DIM:lo + QK_AUG] = aux[:, width + hd * HEAD_DIM:
                                                   width + (hd + 1) * HEAD_DIM].astype(BF16)
    v_ref[...] = _dot(xh, w_ref[:, 2 * width:3 * width]).astype(BF16)
    g_ref[...] = _silu(_dot(xh, w_ref[:, 3 * width:4 * width])).astype(BF16)


def _forget_expand_matrix():
    e = np.zeros((LANES, 2 * HEADS * HEAD_DIM), np.float32)
    one_row = 3 * FORGET_SPLIT
    for hd in range(HEADS):
        qcol = hd * HEAD_DIM
        kcol = HEADS * HEAD_DIM + hd * HEAD_DIM
        for j in range(3):
            e[one_row, qcol + j] = 1.0
            e[j * FORGET_SPLIT + hd, qcol + 3 + j] = 1.0
            e[j * FORGET_SPLIT + hd, kcol + j] = -1.0
            e[one_row, kcol + 3 + j] = 1.0
    return e


def _fox_inproj(h, w_in, b_f, n_tiles):
    batch = h.shape[0]
    width = HEADS * HEAD_DIM
    w_main = w_in[:, :4 * width].astype(BF16)
    w_f = w_in[:, 4 * width:]
    w_f3 = jnp.zeros((D_MODEL, LANES), F32)
    b_f3 = jnp.zeros((1, LANES), F32)
    for j in range(3):
        w_f3 = w_f3.at[:, j * FORGET_SPLIT:j * FORGET_SPLIT + HEADS].set(w_f)
        b_f3 = b_f3.at[0, j * FORGET_SPLIT:j * FORGET_SPLIT + HEADS].set(b_f)
    wf_hi = w_f3.astype(BF16)
    wf_lo = (w_f3 - wf_hi.astype(F32)).astype(BF16)
    tri = jnp.asarray(np.tril(np.ones((TILE, TILE), np.float32)))
    expand = jnp.asarray(_forget_expand_matrix(), dtype=BF16)

    row_spec = lambda n: pl.BlockSpec((None, TILE, n), lambda b, i: (b, i, 0))
    lp = h.shape[1]
    out_shape = (jax.ShapeDtypeStruct((batch, lp, HEADS * QK_AUG), BF16),
                 jax.ShapeDtypeStruct((batch, lp, HEADS * QK_AUG), BF16),
                 jax.ShapeDtypeStruct((batch, lp, width), BF16),
                 jax.ShapeDtypeStruct((batch, lp, width), BF16))
    return pl.pallas_call(
        _fox_inproj_kernel,
        grid=(batch, n_tiles),
        in_specs=[row_spec(D_MODEL), _resident(w_main.shape), _resident(wf_hi.shape),
                  _resident(wf_lo.shape), _resident(b_f3.shape), _resident(tri.shape),
                  _resident(expand.shape)],
        out_specs=(row_spec(HEADS * QK_AUG), row_spec(HEADS * QK_AUG),
                   row_spec(width), row_spec(width)),
        out_shape=out_shape,
        scratch_shapes=[pltpu.VMEM((1, LANES), F32)],
        compiler_params=_params("arbitrary", "arbitrary"),
        name="fox_inproj",
    )(h, w_main, wf_hi, wf_lo, b_f3, tri, expand)


def _rms(x, gain):
    ms = jnp.mean(jnp.square(x), axis=-1, keepdims=True)
    return x * lax.rsqrt(ms + NORM_EPS) * gain


def _mla_inproj_kernel(h_ref, w_ref, qg_ref, kvg_ref, wuq_ref, wukv_ref, cos_ref, sin_ref,
                       qa_ref, ka_ref, v_ref, g_ref):
    xh = h_ref[...].astype(BF16)
    cos = cos_ref[...]
    sin = sin_ref[...]
    n_low = MLA_Q_LORA + MLA_KV_LORA
    low = _dot(xh, w_ref[:, 0:n_low + 2 * LANES])
    cq = _rms(low[:, 0:MLA_Q_LORA], qg_ref[...]) * ((HEAD_DIM + MLA_ROPE) ** -0.5)
    ckv = _rms(low[:, MLA_Q_LORA:n_low], kvg_ref[...])
    k_rope = (low[:, n_low:n_low + LANES] * cos + low[:, n_low + LANES:n_low + 2 * LANES] * sin)
    k_rope = k_rope.astype(BF16)

    cqb = cq.astype(BF16)
    q_main = _dot(cqb, wuq_ref[:, 0:HEADS * QK_AUG])
    q_rot = _dot(cqb, wuq_ref[:, HEADS * QK_AUG:])
    kv = _dot(ckv.astype(BF16), wukv_ref[...])
    for hd in range(HEADS):
        lo = hd * QK_AUG
        qa_ref[:, lo:lo + HEAD_DIM] = q_main[:, lo:lo + HEAD_DIM].astype(BF16)
        qa_ref[:, lo + HEAD_DIM:lo + QK_AUG] = (
            q_main[:, lo + HEAD_DIM:lo + QK_AUG] * cos
            + q_rot[:, hd * LANES:(hd + 1) * LANES] * sin).astype(BF16)
        ka_ref[:, lo:lo + HEAD_DIM] = kv[:, lo:lo + HEAD_DIM].astype(BF16)
        ka_ref[:, lo + HEAD_DIM:lo + QK_AUG] = k_rope
        v_ref[:, hd * HEAD_DIM:(hd + 1) * HEAD_DIM] = kv[:, lo + HEAD_DIM:lo + QK_AUG].astype(BF16)
    g_ref[...] = _silu(_dot(xh, w_ref[:, n_low + 2 * LANES:])).astype(BF16)


def _rot_cols(w):
    half = w.shape[-1] // 2
    return jnp.concatenate([-w[..., half:], w[..., :half]], axis=-1)


def _mla_inproj(h, pos, w_in, q_norm, kv_norm, w_uq, w_ukv, n_tiles):
    batch, lp, _ = h.shape
    width = HEADS * HEAD_DIM
    n_low = MLA_Q_LORA + MLA_KV_LORA
    zeros64 = jnp.zeros((D_MODEL, LANES - MLA_ROPE), F32)
    w_kr = w_in[:, n_low:n_low + MLA_ROPE]
    w_aug = jnp.concatenate(
        [w_in[:, :n_low], w_kr, zeros64, _rot_cols(w_kr), zeros64,
         w_in[:, n_low + MLA_ROPE:]], axis=1).astype(BF16)
    per_head = HEAD_DIM + MLA_ROPE
    wq = w_uq.reshape(MLA_Q_LORA, HEADS, per_head)
    zq = jnp.zeros((MLA_Q_LORA, HEADS, LANES - MLA_ROPE), F32)
    wq_main = jnp.concatenate([wq, zq], axis=-1).reshape(MLA_Q_LORA, HEADS * QK_AUG)
    wq_rot = jnp.concatenate([_rot_cols(wq[..., HEAD_DIM:]), zq], axis=-1)
    wq_rot = wq_rot.reshape(MLA_Q_LORA, HEADS * LANES)
    wuq_aug = jnp.concatenate([wq_main, wq_rot], axis=1).astype(BF16)

    inv_freq = ROPE_BASE ** (-jnp.arange(0, MLA_ROPE, 2, dtype=F32) / MLA_ROPE)
    ang = pos[:, None] * inv_freq[None, :]
    ztab = jnp.zeros((lp, LANES - MLA_ROPE), F32)
    cos_tab = jnp.concatenate([jnp.cos(ang), jnp.cos(ang), ztab], axis=1)
    sin_tab = jnp.concatenate([jnp.sin(ang), jnp.sin(ang), ztab], axis=1)

    row_spec = lambda n: pl.BlockSpec((None, TILE, n), lambda b, i: (b, i, 0))
    tab_spec = pl.BlockSpec((TILE, LANES), lambda b, i: (i, 0))
    out_shape = (jax.ShapeDtypeStruct((batch, lp, HEADS * QK_AUG), BF16),
                 jax.ShapeDtypeStruct((batch, lp, HEADS * QK_AUG), BF16),
                 jax.ShapeDtypeStruct((batch, lp, width), BF16),
                 jax.ShapeDtypeStruct((batch, lp, width), BF16))
    return pl.pallas_call(
        _mla_inproj_kernel,
        grid=(batch, n_tiles),
        in_specs=[row_spec(D_MODEL), _resident(w_aug.shape), _resident((1, MLA_Q_LORA)),
                  _resident((1, MLA_KV_LORA)), _resident(wuq_aug.shape),
                  _resident((MLA_KV_LORA, 2 * width)), tab_spec, tab_spec],
        out_specs=(row_spec(HEADS * QK_AUG), row_spec(HEADS * QK_AUG),
                   row_spec(width), row_spec(width)),
        out_shape=out_shape,
        compiler_params=_params("arbitrary", "arbitrary"),
        name="mla_inproj",
    )(h, w_aug, q_norm.reshape(1, -1), kv_norm.reshape(1, -1), wuq_aug,
      w_ukv.astype(BF16), cos_tab, sin_tab)


def _attention_kernel(qa_ref, ka_ref, v_ref, g_ref, o_ref, vt_ref, m_ref, l_ref, acc_ref):
    i = pl.program_id(2)
    n_chunks = vt_ref.shape[0]

    @pl.when(i == 0)
    def _():
        def transpose_chunk(c, carry):
            start = pl.multiple_of(c * TILE, TILE)
            vt_ref[c] = v_ref[pl.ds(start, TILE), :].astype(F32).T.astype(BF16)
            return carry
        lax.fori_loop(0, n_chunks, transpose_chunk, 0)

    q = qa_ref[...]

    def scores(k_chunk):
        return _dot_nt(k_chunk, q)

    s = scores(ka_ref[PAD - META_CHUNK:PAD, :])
    key_row = (PAD - META_CHUNK) + lax.broadcasted_iota(jnp.int32, s.shape, 0)
    q_row = i * TILE + lax.broadcasted_iota(jnp.int32, s.shape, 1)
    s = jnp.where((key_row >= META0) & (key_row <= q_row), s, NEG_INF)
    m0 = jnp.max(s, axis=0, keepdims=True)
    p = jnp.exp(s - m0)
    m_ref[...] = m0
    l_ref[...] = jnp.sum(p, axis=0, keepdims=True)
    acc_ref[...] = _dot(vt_ref[0][:, TILE - META_CHUNK:], p.astype(BF16))

    def update(s, vt_chunk):
        m_prev = m_ref[...]
        m_new = jnp.maximum(m_prev, jnp.max(s, axis=0, keepdims=True))
        alpha = jnp.exp(m_prev - m_new)
        p = jnp.exp(s - m_new)
        m_ref[...] = m_new
        l_ref[...] = alpha * l_ref[...] + jnp.sum(p, axis=0, keepdims=True)
        acc_ref[...] = alpha * acc_ref[...] + _dot(vt_chunk, p.astype(BF16))

    def full_chunk(j, carry):
        start = pl.multiple_of(j * TILE, TILE)
        update(scores(ka_ref[pl.ds(start, TILE), :]), vt_ref[j])
        return carry

    lax.fori_loop(1, i, full_chunk, 0)

    @pl.when(i >= 1)
    def _():
        start = pl.multiple_of(i * TILE, TILE)
        s = scores(ka_ref[pl.ds(start, TILE), :])
        key = lax.broadcasted_iota(jnp.int32, s.shape, 0)
        qry = lax.broadcasted_iota(jnp.int32, s.shape, 1)
        update(jnp.where(key <= qry, s, NEG_INF), vt_ref[i])

    out_t = acc_ref[...] * (1.0 / l_ref[...])
    o_ref[...] = (out_t.T * g_ref[...].astype(F32)).astype(BF16)


def _attention(qa, ka, v, g):
    batch, lp, _ = qa.shape
    n_tiles = lp // TILE
    head_spec = lambda n: pl.BlockSpec((None, TILE, n), lambda b, h, i: (b, i, h))
    full_spec = lambda n: pl.BlockSpec((None, lp, n), lambda b, h, i: (b, 0, h))
    return pl.pallas_call(
        _attention_kernel,
        grid=(batch, HEADS, n_tiles),
        in_specs=[head_spec(QK_AUG), full_spec(QK_AUG), full_spec(HEAD_DIM),
                  head_spec(HEAD_DIM)],
        out_specs=head_spec(HEAD_DIM),
        out_shape=jax.ShapeDtypeStruct((batch, lp, HEADS * HEAD_DIM), BF16),
        scratch_shapes=[pltpu.VMEM((n_tiles, HEAD_DIM, TILE), BF16),
                        pltpu.VMEM((1, TILE), F32), pltpu.VMEM((1, TILE), F32),
                        pltpu.VMEM((HEAD_DIM, TILE), F32)],
        compiler_params=_params("arbitrary", "arbitrary", "arbitrary"),
        name="causal_attention",
    )(qa, ka, v, g)


def _ret_inproj_kernel(h_ref, w_ref, cos_ref, sin_ref, q_ref, k_ref, v_ref, g_ref):
    xh = h_ref[...].astype(BF16)
    cos = cos_ref[...]
    sin = sin_ref[...]
    qk_width = RET_HEADS * RET_QK
    v_width = RET_HEADS * RET_V
    half = RET_QK // 2
    for out_ref, col0, scale in ((q_ref, 0, 1.0), (k_ref, qk_width, RET_QK ** -0.5)):
        t = _dot(xh, w_ref[:, col0:col0 + qk_width])
        for hd in range(RET_HEADS):
            t1 = t[:, hd * RET_QK:hd * RET_QK + half]
            t2 = t[:, hd * RET_QK + half:(hd + 1) * RET_QK]
            out_ref[:, hd * RET_QK:hd * RET_QK + half] = (
                (t1 * cos - t2 * sin) * scale).astype(BF16)
            out_ref[:, hd * RET_QK + half:(hd + 1) * RET_QK] = (
                (t2 * cos + t1 * sin) * scale).astype(BF16)
    v_ref[...] = _dot(xh, w_ref[:, 2 * qk_width:2 * qk_width + v_width]).astype(BF16)
    g_ref[...] = _silu(_dot(xh, w_ref[:, 2 * qk_width + v_width:])).astype(BF16)


def _ret_inproj(h, pos, w_in, n_tiles):
    batch, lp, _ = h.shape
    qk_width = RET_HEADS * RET_QK
    v_width = RET_HEADS * RET_V
    inv_freq = 1.0 / (ROPE_BASE ** jnp.linspace(0.0, 1.0, RET_QK // 2, dtype=F32))
    ang = pos[:, None] * inv_freq[None, :]
    row_spec = lambda n: pl.BlockSpec((None, TILE, n), lambda b, i: (b, i, 0))
    tab_spec = pl.BlockSpec((TILE, LANES), lambda b, i: (i, 0))
    out_shape = (jax.ShapeDtypeStruct((batch, lp, qk_width), BF16),
                 jax.ShapeDtypeStruct((batch, lp, qk_width), BF16),
                 jax.ShapeDtypeStruct((batch, lp, v_width), BF16),
                 jax.ShapeDtypeStruct((batch, lp, v_width), BF16))
    return pl.pallas_call(
        _ret_inproj_kernel,
        grid=(batch, n_tiles),
        in_specs=[row_spec(D_MODEL), _resident(w_in.shape), tab_spec, tab_spec],
        out_specs=(row_spec(qk_width), row_spec(qk_width), row_spec(v_width),
                   row_spec(v_width)),
        out_shape=out_shape,
        compiler_params=_params("arbitrary", "arbitrary"),
        name="ret_inproj",
    )(h, w_in.astype(BF16), jnp.cos(ang), jnp.sin(ang))


def _retention_kernel(q_ref, k_ref, v_ref, g_ref, gn_ref, dmask_ref, qdec_ref, kdec_ref,
                      cdec_ref, o_ref, state_ref):
    c = pl.program_id(2)

    @pl.when(c == 0)
    def _():
        state_ref[...] = jnp.zeros_like(state_ref)

    q = q_ref[...]
    k = k_ref[...]
    row = c * RET_CHUNK + lax.broadcasted_iota(jnp.int32, (RET_CHUNK, RET_V), 0)
    v = jnp.where(row >= META0, v_ref[...].astype(F32), 0.0).astype(BF16)

    s = _dot_nt(q, k) * dmask_ref[...]
    intra = _dot(s.astype(BF16), v)
    state = state_ref[...]
    cross = _dot(q, state.astype(BF16))
    qdec = qdec_ref[...]
    kdec = kdec_ref[...]
    reps = RET_V // LANES
    o = intra + cross * jnp.concatenate([qdec] * reps, axis=1)

    kd = (k.astype(F32) * jnp.concatenate([kdec] * (RET_QK // LANES), axis=1)).astype(BF16)
    kv = lax.dot_general(kd, v, (((0,), (0,)), ((), ())), preferred_element_type=F32)
    state_ref[...] = state * cdec_ref[...] + kv

    mu = jnp.mean(o, axis=-1, keepdims=True)
    var = jnp.mean(jnp.square(o - mu), axis=-1, keepdims=True)
    o = (o - mu) * lax.rsqrt(var + NORM_EPS) * gn_ref[...]
    o_ref[...] = (o * g_ref[...].astype(F32)).astype(BF16)


def _retention(q, k, v, g, gn_g):
    batch, lp, _ = q.shape
    n_chunks = lp // RET_CHUNK
    log_gamma = jnp.log1p(-jnp.exp2(-5.0 - jnp.arange(RET_HEADS, dtype=F32)))
    idx = jnp.arange(RET_CHUNK, dtype=F32)
    rel = idx[:, None] - idx[None, :]
    dmask = jnp.where(rel[None] >= 0, jnp.exp(rel[None] * log_gamma[:, None, None]), 0.0)
    qdec = jnp.exp((idx[None, :] + 1.0) * log_gamma[:, None])
    kdec = jnp.exp((RET_CHUNK - 1.0 - idx)[None, :] * log_gamma[:, None])
    cdec = jnp.exp(RET_CHUNK * log_gamma)
    lanes = lambda t: jnp.broadcast_to(t[..., None], t.shape + (LANES,))
    qdec, kdec = lanes(qdec), lanes(kdec)
    cdec = jnp.broadcast_to(cdec[:, None, None], (RET_HEADS, 1, RET_V))

    chunk_spec = lambda n: pl.BlockSpec((None, RET_CHUNK, n), lambda b, h, c: (b, c, h))
    head_spec = lambda *shape: pl.BlockSpec((None,) + shape, lambda b, h, c: (h,) + (0,) * len(shape))
    return pl.pallas_call(
        _retention_kernel,
        grid=(batch, RET_HEADS, n_chunks),
        in_specs=[chunk_spec(RET_QK), chunk_spec(RET_QK), chunk_spec(RET_V), chunk_spec(RET_V),
                  pl.BlockSpec((1, RET_V), lambda b, h, c: (0, h)),
                  head_spec(RET_CHUNK, RET_CHUNK), head_spec(RET_CHUNK, LANES),
                  head_spec(RET_CHUNK, LANES), head_spec(1, RET_V)],
        out_specs=chunk_spec(RET_V),
        out_shape=jax.ShapeDtypeStruct((batch, lp, RET_HEADS * RET_V), BF16),
        scratch_shapes=[pltpu.VMEM((RET_QK, RET_V), F32)],
        compiler_params=_params("arbitrary", "arbitrary", "arbitrary"),
        name="retention",
    )(q, k, v, g, gn_g.reshape(1, -1), dmask, qdec, kdec, cdec)


def _outproj_ln_kernel(y_ref, w_ref, h_ref, g_ref, b_ref, o_ref):
    t = DEEPNORM_ALPHA * h_ref[...] + _dot(y_ref[...], w_ref[...])
    mu = jnp.mean(t, axis=-1, keepdims=True)
    var = jnp.mean(jnp.square(t - mu), axis=-1, keepdims=True)
    o_ref[...] = (t - mu) * lax.rsqrt(var + NORM_EPS) * g_ref[...] + b_ref[...]


def _outproj_ln(y, w_out, h, ln_g, ln_b, n_tiles, drop_front):
    batch, lp, width = y.shape
    row_spec = lambda n: pl.BlockSpec((None, TILE, n), lambda b, i: (b, i, 0))
    if drop_front:
        out_rows = lp - PAD
        out_spec = pl.BlockSpec((None, TILE, D_MODEL),
                                lambda b, i: (b, jnp.maximum(i - 1, 0), 0))
    else:
        out_rows = lp
        out_spec = row_spec(D_MODEL)
    return pl.pallas_call(
        _outproj_ln_kernel,
        grid=(batch, n_tiles),
        in_specs=[row_spec(width), _resident((width, D_MODEL)), row_spec(D_MODEL),
                  _resident((1, D_MODEL)), _resident((1, D_MODEL))],
        out_specs=out_spec,
        out_shape=jax.ShapeDtypeStruct((batch, out_rows, D_MODEL), F32),
        compiler_params=_params("arbitrary", "arbitrary"),
        name="outproj_layernorm",
    )(y, w_out.astype(BF16), h, ln_g.reshape(1, -1), ln_b.reshape(1, -1))


def kernel(x, meta, fox_w_in, fox_b_f, fox_w_out, mla_w_in, mla_q_norm, mla_kv_norm,
           mla_w_uq, mla_w_ukv, mla_w_out, ret_w_in, ret_gn_g, ret_w_out, ln_g, ln_b):
    batch, seq, _ = x.shape
    assert seq % TILE == 0
    lp = PAD + seq
    n_tiles = lp // TILE
    meta_b = jnp.broadcast_to(meta[None].astype(x.dtype), (batch, N_META, D_MODEL))
    h = jnp.concatenate([jnp.zeros((batch, META0, D_MODEL), x.dtype), meta_b, x], axis=1)
    pos = (jnp.arange(lp) - META0).astype(F32)

    for layer in range(DEPTH):
        kind, j = layer % 3, layer // 3
        if kind == 0:
            qa, ka, v, g = _fox_inproj(h, fox_w_in[j], fox_b_f[j], n_tiles)
            y, w_out = _attention(qa, ka, v, g), fox_w_out[j]
        elif kind == 1:
            qa, ka, v, g = _mla_inproj(h, pos, mla_w_in[j], mla_q_norm[j], mla_kv_norm[j],
                                       mla_w_uq[j], mla_w_ukv[j], n_tiles)
            y, w_out = _attention(qa, ka, v, g), mla_w_out[j]
        else:
            q, k, v, g = _ret_inproj(h, pos, ret_w_in[j], n_tiles)
            y, w_out = _retention(q, k, v, g, ret_gn_g[j]), ret_w_out[j]
        h = _outproj_ln(y, w_out, h, ln_g[layer], ln_b[layer], n_tiles,
                        drop_front=(layer == DEPTH - 1))
    return h
```

```python
import functools

import numpy as np

import jax
import jax.numpy as jnp
from jax import lax
from jax.experimental import pallas as pl
from jax.experimental.pallas import tpu as pltpu

F32 = jnp.float32
BF16 = jnp.bfloat16

D_MODEL = 1024
DEPTH = 4
N_META = 16
HEADS = 8
HEAD_DIM = 128
MLA_ROPE = 64
MLA_Q_LORA = 384
MLA_KV_LORA = 256
RET_HEADS = 4
RET_QK = 256
RET_V = 512
FORGET_SPLIT = 8
ROPE_BASE = 10000.0
DEEPNORM_ALPHA = (2 * DEPTH) ** 0.25
NORM_EPS = 1e-5
NEG_INF = -1e30
LOG2E = 1.4426950408889634

LANES = 128
QK_AUG = 256
TILE = 512
Q_TILE = 2 * TILE
PAD = Q_TILE
META0 = PAD - N_META
META_CHUNK = 128
RET_CHUNK = 256
VMEM_LIMIT_BYTES = 56 * 1024 * 1024


def _params(*semantics):
    return pltpu.CompilerParams(dimension_semantics=semantics,
                                vmem_limit_bytes=VMEM_LIMIT_BYTES)


def _resident(shape):
    return pl.BlockSpec(shape, lambda *_: (0,) * len(shape))


def _dot(a, b):
    return jnp.dot(a, b, preferred_element_type=F32)


def _dot_nt(a, b):
    return lax.dot_general(a, b, (((1,), (1,)), ((), ())), preferred_element_type=F32)


def _silu(z):
    return z / (1.0 + jnp.exp(-z))


def _fox_inproj_kernel(h_ref, w_ref, wfh_ref, wfl_ref, bf_ref, tri_ref, e_ref,
                       qa_ref, ka_ref, v_ref, g_ref, carry_ref):
    i = pl.program_id(1)

    @pl.when(i == 0)
    def _():
        carry_ref[...] = jnp.zeros_like(carry_ref)

    x = h_ref[...]
    xh = x.astype(BF16)
    xl = (x - xh.astype(F32)).astype(BF16)

    f = _dot(xh, wfh_ref[...]) + _dot(xl, wfh_ref[...]) + _dot(xh, wfl_ref[...])
    f = f + bf_ref[...]
    log_f = jnp.minimum(f, 0.0) - jnp.log1p(jnp.exp(-jnp.abs(f)))
    row = i * TILE + lax.broadcasted_iota(jnp.int32, log_f.shape, 0)
    log_f = jnp.where(row >= META0, log_f, 0.0)
    c = jnp.dot(tri_ref[...], log_f, precision=lax.Precision.HIGHEST,
                preferred_element_type=F32) + carry_ref[...]
    carry_ref[...] = c[TILE - 1:TILE, :]

    c = c * LOG2E
    p1 = c.astype(BF16).astype(F32)
    r1 = c - p1
    p2 = r1.astype(BF16).astype(F32)
    p3 = (r1 - p2).astype(BF16).astype(F32)
    lane = lax.broadcasted_iota(jnp.int32, c.shape, 1)
    pieces = jnp.where(lane < FORGET_SPLIT, p1,
                       jnp.where(lane < 2 * FORGET_SPLIT, p2,
                                 jnp.where(lane < 3 * FORGET_SPLIT, p3,
                                           jnp.where(lane == 3 * FORGET_SPLIT, 1.0, 0.0))))
    aux = _dot(pieces.astype(BF16), e_ref[...])

    width = HEADS * HEAD_DIM
    q = _dot(xh, w_ref[:, 0:width]) * (HEAD_DIM ** -0.5 * LOG2E)
    k = _dot(xh, w_ref[:, width:2 * width])
    for hd in range(HEADS):
        src = slice(hd * HEAD_DIM, (hd + 1) * HEAD_DIM)
        lo = hd * QK_AUG
        qa_ref[:, lo:lo + HEAD_DIM] = q[:, src].astype(BF16)
        qa_ref[:, lo + HEAD_DIM:lo + QK_AUG] = aux[:, src].astype(BF16)
        ka_ref[:, lo:lo + HEAD_DIM] = k[:, src].astype(BF16)
        ka_ref[:, lo + HEAD_DIM:lo + QK_AUG] = aux[:, width + hd * HEAD_DIM:
                                                   width + (hd + 1) * HEAD_DIM].astype(BF16)
    v_ref[...] = _dot(xh, w_ref[:, 2 * width:3 * width]).astype(BF16)
    g_ref[...] = _silu(_dot(xh, w_ref[:, 3 * width:4 * width])).astype(BF16)


def _forget_expand_matrix():
    e = np.zeros((LANES, 2 * HEADS * HEAD_DIM), np.float32)
    one_row = 3 * FORGET_SPLIT
    for hd in range(HEADS):
        qcol = hd * HEAD_DIM
        kcol = HEADS * HEAD_DIM + hd * HEAD_DIM
        for j in range(3):
            e[one_row, qcol + j] = 1.0
            e[j * FORGET_SPLIT + hd, qcol + 3 + j] = 1.0
            e[j * FORGET_SPLIT + hd, kcol + j] = -1.0
            e[one_row, kcol + 3 + j] = 1.0
    return e


def _fox_inproj(h, w_in, b_f, n_tiles):
    batch = h.shape[0]
    width = HEADS * HEAD_DIM
    w_main = w_in[:, :4 * width].astype(BF16)
    w_f = w_in[:, 4 * width:]
    w_f3 = jnp.zeros((D_MODEL, LANES), F32)
    b_f3 = jnp.zeros((1, LANES), F32)
    for j in range(3):
        w_f3 = w_f3.at[:, j * FORGET_SPLIT:j * FORGET_SPLIT + HEADS].set(w_f)
        b_f3 = b_f3.at[0, j * FORGET_SPLIT:j * FORGET_SPLIT + HEADS].set(b_f)
    wf_hi = w_f3.astype(BF16)
    wf_lo = (w_f3 - wf_hi.astype(F32)).astype(BF16)
    tri = jnp.asarray(np.tril(np.ones((TILE, TILE), np.float32)))
    expand = jnp.asarray(_forget_expand_matrix(), dtype=BF16)

    row_spec = lambda n: pl.BlockSpec((None, TILE, n), lambda b, i: (b, i, 0))
    lp = h.shape[1]
    out_shape = (jax.ShapeDtypeStruct((batch, lp, HEADS * QK_AUG), BF16),
                 jax.ShapeDtypeStruct((batch, lp, HEADS * QK_AUG), BF16),
                 jax.ShapeDtypeStruct((batch, lp, width), BF16),
                 jax.ShapeDtypeStruct((batch, lp, width), BF16))
    return pl.pallas_call(
        _fox_inproj_kernel,
        grid=(batch, n_tiles),
        in_specs=[row_spec(D_MODEL), _resident(w_main.shape), _resident(wf_hi.shape),
                  _resident(wf_lo.shape), _resident(b_f3.shape), _resident(tri.shape),
                  _resident(expand.shape)],
        out_specs=(row_spec(HEADS * QK_AUG), row_spec(HEADS * QK_AUG),
                   row_spec(width), row_spec(width)),
        out_shape=out_shape,
        scratch_shapes=[pltpu.VMEM((1, LANES), F32)],
        compiler_params=_params("arbitrary", "arbitrary"),
        name="fox_inproj",
    )(h, w_main, wf_hi, wf_lo, b_f3, tri, expand)


def _rms(x, gain):
    ms = jnp.mean(jnp.square(x), axis=-1, keepdims=True)
    return x * lax.rsqrt(ms + NORM_EPS) * gain


def _mla_inproj_kernel(h_ref, w_ref, qg_ref, kvg_ref, wuq_ref, wukv_ref, cos_ref, sin_ref,
                       qa_ref, ka_ref, v_ref, g_ref):
    xh = h_ref[...].astype(BF16)
    cos = cos_ref[...]
    sin = sin_ref[...]
    n_low = MLA_Q_LORA + MLA_KV_LORA
    low = _dot(xh, w_ref[:, 0:n_low + 2 * LANES])
    cq = _rms(low[:, 0:MLA_Q_LORA], qg_ref[...]) * ((HEAD_DIM + MLA_ROPE) ** -0.5 * LOG2E)
    ckv = _rms(low[:, MLA_Q_LORA:n_low], kvg_ref[...])
    k_rope = (low[:, n_low:n_low + LANES] * cos + low[:, n_low + LANES:n_low + 2 * LANES] * sin)
    k_rope = k_rope.astype(BF16)

    cqb = cq.astype(BF16)
    q_main = _dot(cqb, wuq_ref[:, 0:HEADS * QK_AUG])
    q_rot = _dot(cqb, wuq_ref[:, HEADS * QK_AUG:])
    kv = _dot(ckv.astype(BF16), wukv_ref[...])
    for hd in range(HEADS):
        lo = hd * QK_AUG
        qa_ref[:, lo:lo + HEAD_DIM] = q_main[:, lo:lo + HEAD_DIM].astype(BF16)
        qa_ref[:, lo + HEAD_DIM:lo + QK_AUG] = (
            q_main[:, lo + HEAD_DIM:lo + QK_AUG] * cos
            + q_rot[:, hd * LANES:(hd + 1) * LANES] * sin).astype(BF16)
        ka_ref[:, lo:lo + HEAD_DIM] = kv[:, lo:lo + HEAD_DIM].astype(BF16)
        ka_ref[:, lo + HEAD_DIM:lo + QK_AUG] = k_rope
        v_ref[:, hd * HEAD_DIM:(hd + 1) * HEAD_DIM] = kv[:, lo + HEAD_DIM:lo + QK_AUG].astype(BF16)
    g_ref[...] = _silu(_dot(xh, w_ref[:, n_low + 2 * LANES:])).astype(BF16)


def _rot_cols(w):
    half = w.shape[-1] // 2
    return jnp.concatenate([-w[..., half:], w[..., :half]], axis=-1)


def _mla_inproj(h, pos, w_in, q_norm, kv_norm, w_uq, w_ukv, n_tiles):
    batch, lp, _ = h.shape
    width = HEADS * HEAD_DIM
    n_low = MLA_Q_LORA + MLA_KV_LORA
    zeros64 = jnp.zeros((D_MODEL, LANES - MLA_ROPE), F32)
    w_kr = w_in[:, n_low:n_low + MLA_ROPE]
    w_aug = jnp.concatenate(
        [w_in[:, :n_low], w_kr, zeros64, _rot_cols(w_kr), zeros64,
         w_in[:, n_low + MLA_ROPE:]], axis=1).astype(BF16)
    per_head = HEAD_DIM + MLA_ROPE
    wq = w_uq.reshape(MLA_Q_LORA, HEADS, per_head)
    zq = jnp.zeros((MLA_Q_LORA, HEADS, LANES - MLA_ROPE), F32)
    wq_main = jnp.concatenate([wq, zq], axis=-1).reshape(MLA_Q_LORA, HEADS * QK_AUG)
    wq_rot = jnp.concatenate([_rot_cols(wq[..., HEAD_DIM:]), zq], axis=-1)
    wq_rot = wq_rot.reshape(MLA_Q_LORA, HEADS * LANES)
    wuq_aug = jnp.concatenate([wq_main, wq_rot], axis=1).astype(BF16)

    inv_freq = ROPE_BASE ** (-jnp.arange(0, MLA_ROPE, 2, dtype=F32) / MLA_ROPE)
    ang = pos[:, None] * inv_freq[None, :]
    ztab = jnp.zeros((lp, LANES - MLA_ROPE), F32)
    cos_tab = jnp.concatenate([jnp.cos(ang), jnp.cos(ang), ztab], axis=1)
    sin_tab = jnp.concatenate([jnp.sin(ang), jnp.sin(ang), ztab], axis=1)

    row_spec = lambda n: pl.BlockSpec((None, TILE, n), lambda b, i: (b, i, 0))
    tab_spec = pl.BlockSpec((TILE, LANES), lambda b, i: (i, 0))
    out_shape = (jax.ShapeDtypeStruct((batch, lp, HEADS * QK_AUG), BF16),
                 jax.ShapeDtypeStruct((batch, lp, HEADS * QK_AUG), BF16),
                 jax.ShapeDtypeStruct((batch, lp, width), BF16),
                 jax.ShapeDtypeStruct((batch, lp, width), BF16))
    return pl.pallas_call(
        _mla_inproj_kernel,
        grid=(batch, n_tiles),
        in_specs=[row_spec(D_MODEL), _resident(w_aug.shape), _resident((1, MLA_Q_LORA)),
                  _resident((1, MLA_KV_LORA)), _resident(wuq_aug.shape),
                  _resident((MLA_KV_LORA, 2 * width)), tab_spec, tab_spec],
        out_specs=(row_spec(HEADS * QK_AUG), row_spec(HEADS * QK_AUG),
                   row_spec(width), row_spec(width)),
        out_shape=out_shape,
        compiler_params=_params("arbitrary", "arbitrary"),
        name="mla_inproj",
    )(h, w_aug, q_norm.reshape(1, -1), kv_norm.reshape(1, -1), wuq_aug,
      w_ukv.astype(BF16), cos_tab, sin_tab)


def _attention_kernel(qa_ref, ka_ref, v_ref, g_ref, o_ref,
                      vt_ref, s_ref, mx_ref, m_ref, l_ref, acc_ref):
    i = pl.program_id(2)
    n_key_tiles = vt_ref.shape[0]

    @pl.when(i == 0)
    def _():
        def transpose_tile(c, carry):
            start = pl.multiple_of(c * TILE, TILE)
            vt_ref[c] = v_ref[pl.ds(start, TILE), :].astype(F32).T.astype(BF16)
            return carry
        lax.fori_loop(0, n_key_tiles, transpose_tile, 0)

    q = qa_ref[...]

    s = _dot_nt(ka_ref[PAD - META_CHUNK:PAD, :], q)
    key_row = (PAD - META_CHUNK) + lax.broadcasted_iota(jnp.int32, s.shape, 0)
    q_row = i * Q_TILE + lax.broadcasted_iota(jnp.int32, s.shape, 1)
    s = jnp.where((key_row >= META0) & (key_row <= q_row), s, NEG_INF)
    m0 = jnp.max(s, axis=0, keepdims=True)
    p = jnp.exp2(s - m0)
    m_ref[...] = m0
    l_ref[...] = jnp.sum(p, axis=0, keepdims=True)
    acc_ref[...] = _dot(vt_ref[PAD // TILE - 1][:, TILE - META_CHUNK:], p.astype(BF16))

    def stage_a(c, slot, diagonal_offset=None):
        start = pl.multiple_of(c * TILE, TILE)
        s = _dot_nt(ka_ref[pl.ds(start, TILE), :], q)
        if diagonal_offset is not None:
            key = lax.broadcasted_iota(jnp.int32, s.shape, 0) + diagonal_offset
            qry = lax.broadcasted_iota(jnp.int32, s.shape, 1)
            s = jnp.where(key <= qry, s, NEG_INF)
        s_ref[slot] = s
        mx_ref[slot] = jnp.max(s, axis=0, keepdims=True)

    def stage_b(c, slot):
        m_prev = m_ref[...]
        m_new = jnp.maximum(m_prev, mx_ref[slot])
        alpha = jnp.exp2(m_prev - m_new)
        p = jnp.exp2(s_ref[slot] - m_new)
        m_ref[...] = m_new
        l_ref[...] = alpha * l_ref[...] + jnp.sum(p, axis=0, keepdims=True)
        acc_ref[...] = alpha * acc_ref[...] + _dot(vt_ref[c], p.astype(BF16))

    @pl.when(i >= 1)
    def _():
        first_full = PAD // TILE
        stage_a(2 * i, 0, diagonal_offset=0)
        stage_a(2 * i + 1, 1, diagonal_offset=TILE)
        stage_b(2 * i, 0)

        def pair(k, prev_tile):
            c = first_full + 2 * k
            stage_a(c, 0)
            stage_b(prev_tile, 1)
            stage_a(c + 1, 1)
            stage_b(c, 0)
            return c + 1

        last_tile = lax.fori_loop(0, i - 1, pair, 2 * i + 1)
        stage_b(last_tile, 1)

    out_t = acc_ref[...] * (1.0 / l_ref[...])
    o_ref[...] = (out_t.T * g_ref[...].astype(F32)).astype(BF16)


def _attention(qa, ka, v, g):
    batch, lp, _ = qa.shape
    head_spec = lambda n: pl.BlockSpec((None, Q_TILE, n), lambda b, h, i: (b, i, h))
    full_spec = lambda n: pl.BlockSpec((None, lp, n), lambda b, h, i: (b, 0, h))
    return pl.pallas_call(
        _attention_kernel,
        grid=(batch, HEADS, lp // Q_TILE),
        in_specs=[head_spec(QK_AUG), full_spec(QK_AUG), full_spec(HEAD_DIM),
                  head_spec(HEAD_DIM)],
        out_specs=head_spec(HEAD_DIM),
        out_shape=jax.ShapeDtypeStruct((batch, lp, HEADS * HEAD_DIM), BF16),
        scratch_shapes=[pltpu.VMEM((lp // TILE, HEAD_DIM, TILE), BF16),
                        pltpu.VMEM((2, TILE, Q_TILE), F32),
                        pltpu.VMEM((2, 1, Q_TILE), F32),
                        pltpu.VMEM((1, Q_TILE), F32), pltpu.VMEM((1, Q_TILE), F32),
                        pltpu.VMEM((HEAD_DIM, Q_TILE), F32)],
        compiler_params=_params("arbitrary", "arbitrary", "arbitrary"),
        name="causal_attention",
    )(qa, ka, v, g)


def _ret_inproj_kernel(h_ref, w_ref, cos_ref, sin_ref, q_ref, k_ref, v_ref, g_ref):
    xh = h_ref[...].astype(BF16)
    cos = cos_ref[...]
    sin = sin_ref[...]
    qk_width = RET_HEADS * RET_QK
    v_width = RET_HEADS * RET_V
    half = RET_QK // 2
    for out_ref, col0, scale in ((q_ref, 0, 1.0), (k_ref, qk_width, RET_QK ** -0.5)):
        t = _dot(xh, w_ref[:, col0:col0 + qk_width])
        for hd in range(RET_HEADS):
            t1 = t[:, hd * RET_QK:hd * RET_QK + half]
            t2 = t[:, hd * RET_QK + half:(hd + 1) * RET_QK]
            out_ref[:, hd * RET_QK:hd * RET_QK + half] = (
                (t1 * cos - t2 * sin) * scale).astype(BF16)
            out_ref[:, hd * RET_QK + half:(hd + 1) * RET_QK] = (
                (t2 * cos + t1 * sin) * scale).astype(BF16)
    v_ref[...] = _dot(xh, w_ref[:, 2 * qk_width:2 * qk_width + v_width]).astype(BF16)
    g_ref[...] = _silu(_dot(xh, w_ref[:, 2 * qk_width + v_width:])).astype(BF16)


def _ret_inproj(h, pos, w_in, n_tiles):
    batch, lp, _ = h.shape
    qk_width = RET_HEADS * RET_QK
    v_width = RET_HEADS * RET_V
    inv_freq = 1.0 / (ROPE_BASE ** jnp.linspace(0.0, 1.0, RET_QK // 2, dtype=F32))
    ang = pos[:, None] * inv_freq[None, :]
    row_spec = lambda n: pl.BlockSpec((None, TILE, n), lambda b, i: (b, i, 0))
    tab_spec = pl.BlockSpec((TILE, LANES), lambda b, i: (i, 0))
    out_shape = (jax.ShapeDtypeStruct((batch, lp, qk_width), BF16),
                 jax.ShapeDtypeStruct((batch, lp, qk_width), BF16),
                 jax.ShapeDtypeStruct((batch, lp, v_width), BF16),
                 jax.ShapeDtypeStruct((batch, lp, v_width), BF16))
    return pl.pallas_call(
        _ret_inproj_kernel,
        grid=(batch, n_tiles),
        in_specs=[row_spec(D_MODEL), _resident(w_in.shape), tab_spec, tab_spec],
        out_specs=(row_spec(qk_width), row_spec(qk_width), row_spec(v_width),
                   row_spec(v_width)),
        out_shape=out_shape,
        compiler_params=_params("arbitrary", "arbitrary"),
        name="ret_inproj",
    )(h, w_in.astype(BF16), jnp.cos(ang), jnp.sin(ang))


def _retention_kernel(q_ref, k_ref, v_ref, g_ref, gn_ref, dmask_ref, qdec_ref, kdec_ref,
                      cdec_ref, o_ref, state_ref):
    c = pl.program_id(2)

    @pl.when(c == 0)
    def _():
        state_ref[...] = jnp.zeros_like(state_ref)

    q = q_ref[...]
    k = k_ref[...]
    row = c * RET_CHUNK + lax.broadcasted_iota(jnp.int32, (RET_CHUNK, RET_V), 0)
    v = jnp.where(row >= META0, v_ref[...].astype(F32), 0.0).astype(BF16)

    s = _dot_nt(q, k) * dmask_ref[...]
    intra = _dot(s.astype(BF16), v)
    state = state_ref[...]
    cross = _dot(q, state.astype(BF16))
    qdec = qdec_ref[...]
    kdec = kdec_ref[...]
    reps = RET_V // LANES
    o = intra + cross * jnp.concatenate([qdec] * reps, axis=1)

    kd = (k.astype(F32) * jnp.concatenate([kdec] * (RET_QK // LANES), axis=1)).astype(BF16)
    kv = lax.dot_general(kd, v, (((0,), (0,)), ((), ())), preferred_element_type=F32)
    state_ref[...] = state * cdec_ref[...] + kv

    mu = jnp.mean(o, axis=-1, keepdims=True)
    var = jnp.mean(jnp.square(o - mu), axis=-1, keepdims=True)
    o = (o - mu) * lax.rsqrt(var + NORM_EPS) * gn_ref[...]
    o_ref[...] = (o * g_ref[...].astype(F32)).astype(BF16)


def _retention(q, k, v, g, gn_g):
    batch, lp, _ = q.shape
    n_chunks = lp // RET_CHUNK
    log_gamma = jnp.log1p(-jnp.exp2(-5.0 - jnp.arange(RET_HEADS, dtype=F32)))
    idx = jnp.arange(RET_CHUNK, dtype=F32)
    rel = idx[:, None] - idx[None, :]
    dmask = jnp.where(rel[None] >= 0, jnp.exp(rel[None] * log_gamma[:, None, None]), 0.0)
    qdec = jnp.exp((idx[None, :] + 1.0) * log_gamma[:, None])
    kdec = jnp.exp((RET_CHUNK - 1.0 - idx)[None, :] * log_gamma[:, None])
    cdec = jnp.exp(RET_CHUNK * log_gamma)
    lanes = lambda t: jnp.broadcast_to(t[..., None], t.shape + (LANES,))
    qdec, kdec = lanes(qdec), lanes(kdec)
    cdec = jnp.broadcast_to(cdec[:, None, None], (RET_HEADS, 1, RET_V))

    chunk_spec = lambda n: pl.BlockSpec((None, RET_CHUNK, n), lambda b, h, c: (b, c, h))
    head_spec = lambda *shape: pl.BlockSpec((None,) + shape, lambda b, h, c: (h,) + (0,) * len(shape))
    return pl.pallas_call(
        _retention_kernel,
        grid=(batch, RET_HEADS, n_chunks),
        in_specs=[chunk_spec(RET_QK), chunk_spec(RET_QK), chunk_spec(RET_V), chunk_spec(RET_V),
                  pl.BlockSpec((1, RET_V), lambda b, h, c: (0, h)),
                  head_spec(RET_CHUNK, RET_CHUNK), head_spec(RET_CHUNK, LANES),
                  head_spec(RET_CHUNK, LANES), head_spec(1, RET_V)],
        out_specs=chunk_spec(RET_V),
        out_shape=jax.ShapeDtypeStruct((batch, lp, RET_HEADS * RET_V), BF16),
        scratch_shapes=[pltpu.VMEM((RET_QK, RET_V), F32)],
        compiler_params=_params("arbitrary", "arbitrary", "arbitrary"),
        name="retention",
    )(q, k, v, g, gn_g.reshape(1, -1), dmask, qdec, kdec, cdec)


def _outproj_ln_kernel(y_ref, w_ref, h_ref, g_ref, b_ref, o_ref):
    t = DEEPNORM_ALPHA * h_ref[...] + _dot(y_ref[...], w_ref[...])
    mu = jnp.mean(t, axis=-1, keepdims=True)
    var = jnp.mean(jnp.square(t - mu), axis=-1, keepdims=True)
    o_ref[...] = (t - mu) * lax.rsqrt(var + NORM_EPS) * g_ref[...] + b_ref[...]


def _outproj_ln(y, w_out, h, ln_g, ln_b, n_tiles, drop_front):
    batch, lp, width = y.shape
    row_spec = lambda n: pl.BlockSpec((None, TILE, n), lambda b, i: (b, i, 0))
    if drop_front:
        out_rows = lp - PAD
        out_spec = pl.BlockSpec((None, TILE, D_MODEL),
                                lambda b, i: (b, jnp.maximum(i - PAD // TILE, 0), 0))
    else:
        out_rows = lp
        out_spec = row_spec(D_MODEL)
    return pl.pallas_call(
        _outproj_ln_kernel,
        grid=(batch, n_tiles),
        in_specs=[row_spec(width), _resident((width, D_MODEL)), row_spec(D_MODEL),
                  _resident((1, D_MODEL)), _resident((1, D_MODEL))],
        out_specs=out_spec,
        out_shape=jax.ShapeDtypeStruct((batch, out_rows, D_MODEL), F32),
        compiler_params=_params("arbitrary", "arbitrary"),
        name="outproj_layernorm",
    )(y, w_out.astype(BF16), h, ln_g.reshape(1, -1), ln_b.reshape(1, -1))


def kernel(x, meta, fox_w_in, fox_b_f, fox_w_out, mla_w_in, mla_q_norm, mla_kv_norm,
           mla_w_uq, mla_w_ukv, mla_w_out, ret_w_in, ret_gn_g, ret_w_out, ln_g, ln_b):
    batch, seq, _ = x.shape
    assert seq % Q_TILE == 0
    lp = PAD + seq
    n_tiles = lp // TILE
    meta_b = jnp.broadcast_to(meta[None].astype(x.dtype), (batch, N_META, D_MODEL))
    h = jnp.concatenate([jnp.zeros((batch, META0, D_MODEL), x.dtype), meta_b, x], axis=1)
    pos = (jnp.arange(lp) - META0).astype(F32)

    for layer in range(DEPTH):
        kind, j = layer % 3, layer // 3
        if kind == 0:
            qa, ka, v, g = _fox_inproj(h, fox_w_in[j], fox_b_f[j], n_tiles)
            y, w_out = _attention(qa, ka, v, g), fox_w_out[j]
        elif kind == 1:
            qa, ka, v, g = _mla_inproj(h, pos, mla_w_in[j], mla_q_norm[j], mla_kv_norm[j],
                                       mla_w_uq[j], mla_w_ukv[j], n_tiles)
            y, w_out = _attention(qa, ka, v, g), mla_w_out[j]
        else:
            q, k, v, g = _ret_inproj(h, pos, ret_w_in[j], n_tiles)
            y, w_out = _retention(q, k, v, g, ret_gn_g[j]), ret_w_out[j]
        h = _outproj_ln(y, w_out, h, ln_g[layer], ln_b[layer], n_tiles,
                        drop_front=(layer == DEPTH - 1))
    return h
```
